```python
import math
import jax
import jax.numpy as jnp
from jax import lax
import numpy as np

D_MODEL = 2048
BATCH = 8
SEQ = 2048
DEPTH = 2

PLE_DIM = 256
NORM_EPS = 1e-6
ROPE_THETA = 10000.0
NEG_INF = -1e30
FORCE_SCORE = 1e4
Q_BLOCK = 128

N_BRANCH = 4
MIX_WIDTH = D_MODEL // 2

SSD_D_INNER = MIX_WIDTH
SSD_HEAD_DIM = 64
SSD_N_HEADS = SSD_D_INNER // SSD_HEAD_DIM
SSD_N_GROUPS = 2
SSD_HPG = SSD_N_HEADS // SSD_N_GROUPS
SSD_D_STATE = 128
SSD_CONV = 4
SSD_CHUNK = 128
SSD_CONV_DIM = SSD_D_INNER + 2 * SSD_N_GROUPS * SSD_D_STATE

DIFF_N_HEADS = 8
DIFF_HEAD_DIM = 64
DIFF_V_DIM = 2 * DIFF_HEAD_DIM
DIFF_WIDTH = DIFF_N_HEADS * DIFF_V_DIM

NSA_N_HEADS = 16
NSA_N_KV = 4
NSA_HPG = NSA_N_HEADS // NSA_N_KV
NSA_HEAD_DIM = 64
NSA_WIDTH = NSA_N_HEADS * NSA_HEAD_DIM
NSA_KV_WIDTH = NSA_N_KV * NSA_HEAD_DIM
CMP_BLOCK = 32
CMP_STRIDE = 16
CMP_HIDDEN = 256
SEL_BLOCK = 64
SEL_TOPK = 8
WINDOW = 512

RNN_WIDTH = MIX_WIDTH
RNN_BLOCKS = 16
RNN_BLOCK_DIM = RNN_WIDTH // RNN_BLOCKS
RNN_CONV = 4
RG_C = 8.0

D_FF = 3 * D_MODEL
FFN_CONV = 3

IN_WIDTHS = (
    SSD_D_INNER, SSD_CONV_DIM, SSD_N_HEADS,
    DIFF_WIDTH, DIFF_WIDTH, DIFF_WIDTH,
    NSA_WIDTH, NSA_KV_WIDTH, NSA_KV_WIDTH, NSA_KV_WIDTH,
    NSA_KV_WIDTH, NSA_KV_WIDTH, NSA_KV_WIDTH, NSA_N_HEADS * 3,
    RNN_WIDTH, RNN_WIDTH,
)
D_IN = sum(IN_WIDTHS)
IN_OFFSETS = tuple(sum(IN_WIDTHS[:i + 1]) for i in range(len(IN_WIDTHS) - 1))

kernel_name = 'hybrid_ssd_diffattn_nsa_rglru_block'


def rmsnorm(x, g):
    xf = x.astype(jnp.float32)
    y = xf * lax.rsqrt(jnp.mean(xf * xf, axis=-1, keepdims=True) + NORM_EPS)
    return (y * g.astype(jnp.float32)).astype(x.dtype)


def causal_dwconv(x, w, b):
    k_width = w.shape[0]
    s = x.shape[1]
    xp = jnp.pad(x, ((0, 0), (k_width - 1, 0), (0, 0)))
    y = b
    for k in range(k_width):
        y = y + xp[:, k:k + s] * w[k]
    return y


def rope(x, pos):
    half = x.shape[-1] // 2
    inv_freq = ROPE_THETA ** (-jnp.arange(half, dtype=jnp.float32) / half)
    ang = pos.astype(jnp.float32)[:, None] * inv_freq[None, :]
    shape = (1, x.shape[1]) + (1,) * (x.ndim - 3) + (half,)
    cos = jnp.cos(ang).reshape(shape)
    sin = jnp.sin(ang).reshape(shape)
    xf = x.astype(jnp.float32)
    x1, x2 = xf[..., :half], xf[..., half:]
    return jnp.concatenate([x1 * cos - x2 * sin, x2 * cos + x1 * sin], axis=-1).astype(x.dtype)


def masked_softmax(s, mask):
    s = jnp.where(mask, s.astype(jnp.float32), NEG_INF)
    m = jnp.max(s, axis=-1, keepdims=True)
    e = jnp.where(mask, jnp.exp(s - m), 0.0)
    return e / jnp.maximum(jnp.sum(e, axis=-1, keepdims=True), 1e-30)


def segsum(a):
    t = a.shape[-1]
    ar = jnp.broadcast_to(a[..., :, None], a.shape + (t,))
    ar = jnp.where(jnp.tril(jnp.ones((t, t), dtype=bool), -1), ar, 0.0)
    cs = jnp.cumsum(ar, axis=-2)
    return jnp.where(jnp.tril(jnp.ones((t, t), dtype=bool)), cs, -jnp.inf)


def ssd_mixer(z, xbc, dt, conv_w, conv_b, dt_bias, a_log, d_skip, g_norm):
    bsz, s, _ = z.shape
    f32 = jnp.float32
    nc, cl = s // SSD_CHUNK, SSD_CHUNK
    g, j, hp, n = SSD_N_GROUPS, SSD_HPG, SSD_HEAD_DIM, SSD_D_STATE
    xbc = jax.nn.silu(causal_dwconv(xbc, conv_w, conv_b))
    xs, bm, cm = jnp.split(xbc, [SSD_D_INNER, SSD_D_INNER + g * n], axis=-1)
    xs = xs.astype(f32).reshape(bsz, nc, cl, g, j, hp)
    bm = bm.astype(f32).reshape(bsz, nc, cl, g, n)
    cm = cm.astype(f32).reshape(bsz, nc, cl, g, n)
    dt = jax.nn.softplus(dt.astype(f32) + dt_bias.astype(f32))
    a = -jnp.exp(a_log.astype(f32))
    dt_c = dt.reshape(bsz, nc, cl, g, j)
    xdt = xs * dt_c[..., None]
    a_dt = (dt_c * a.reshape(g, j)).transpose(0, 3, 4, 1, 2)
    acs = jnp.cumsum(a_dt, axis=-1)
    decay_in = jnp.exp(segsum(a_dt))
    cb = jnp.einsum('bclgn,bcsgn->bgcls', cm, bm)
    y_diag = jnp.einsum('bgjcls,bcsgjp->bclgjp', cb[:, :, None] * decay_in, xdt)

    def chunk_step(h, inp):
        c_c, b_c, xdt_c, acs_c = inp
        y_off = jnp.einsum('blgn,bgjpn,bgjl->blgjp', c_c, h, jnp.exp(acs_c))
        decay_st = jnp.exp(acs_c[..., -1:] - acs_c)
        s_c = jnp.einsum('blgn,bgjl,blgjp->bgjpn', b_c, decay_st, xdt_c)
        h = h * jnp.exp(acs_c[..., -1])[..., None, None] + s_c
        return h, y_off

    h0 = jnp.zeros((bsz, g, j, hp, n), f32)
    _, y_off = lax.scan(chunk_step, h0, (cm.transpose(1, 0, 2, 3, 4), bm.transpose(1, 0, 2, 3, 4),
                                         xdt.transpose(1, 0, 2, 3, 4, 5), acs.transpose(3, 0, 1, 2, 4)))
    y_off = y_off.transpose(1, 0, 2, 3, 4, 5)
    y = y_diag + y_off + xs * d_skip.astype(f32).reshape(g, j)[:, :, None]
    y = y.reshape(bsz, s, SSD_D_INNER)
    return rmsnorm(y * jax.nn.silu(z.astype(f32)), g_norm)


def diff_attention(q, k, v, lq1, lk1, lq2, lk2, g_norm, lambda_init):
    bsz, s, _ = q.shape
    h, dh = DIFF_N_HEADS, DIFF_HEAD_DIM
    pos = jnp.arange(s)
    q = rope(q.reshape(bsz, s, h, 2, dh), pos)
    k = rope(k.reshape(bsz, s, h, 2, dh), pos)
    v = v.reshape(bsz, s, h, DIFF_V_DIM)
    f32 = jnp.float32
    lam = (jnp.exp(jnp.sum(lq1.astype(f32) * lk1.astype(f32)))
           - jnp.exp(jnp.sum(lq2.astype(f32) * lk2.astype(f32))) + lambda_init)
    scale = dh ** -0.5
    nb = s // Q_BLOCK
    qb = q.reshape(bsz, nb, Q_BLOCK, h, 2, dh).transpose(1, 0, 2, 3, 4, 5)

    def block(args):
        q_blk, i = args
        sc = jnp.einsum('bqhmd,bkhmd->bhmqk', q_blk, k).astype(f32) * scale
        qpos = i * Q_BLOCK + jnp.arange(Q_BLOCK)
        pr = masked_softmax(sc, qpos[:, None] >= pos[None, :])
        w = pr[:, :, 0] - lam * pr[:, :, 1]
        return jnp.einsum('bhqk,bkhe->bqhe', w.astype(v.dtype), v)

    o = lax.map(block, (qb, jnp.arange(nb)))
    o = o.transpose(1, 0, 2, 3, 4).reshape(bsz, s, h, DIFF_V_DIM)
    o = rmsnorm(o, g_norm) * (1.0 - lambda_init)
    return o.reshape(bsz, s, DIFF_WIDTH)


def nsa_attention(q, k_cmp, v_cmp, k_slc, v_slc, k_win, v_win, gate_logits,
                  pos_cmp, ck_w1, ck_w2, cv_w1, cv_w2):
    bsz, s, _ = q.shape
    g, j, dh = NSA_N_KV, NSA_HPG, NSA_HEAD_DIM
    f32 = jnp.float32
    pos = jnp.arange(s)
    scale = dh ** -0.5
    q = rope(q.reshape(bsz, s, g, j, dh), pos)
    k_cmp = rope(k_cmp.reshape(bsz, s, g, dh), pos)
    k_slc = rope(k_slc.reshape(bsz, s, g, dh), pos)
    k_win = rope(k_win.reshape(bsz, s, g, dh), pos)
    v_cmp = v_cmp.reshape(bsz, s, g, dh)
    v_slc = v_slc.reshape(bsz, s, g, dh)
    v_win = v_win.reshape(bsz, s, g, dh)

    n_cmp = (s - CMP_BLOCK) // CMP_STRIDE + 1
    cidx = np.arange(n_cmp)[:, None] * CMP_STRIDE + np.arange(CMP_BLOCK)[None, :]

    def compress(t, w1, w2):
        blk = t[:, cidx] + pos_cmp[None, None, :, None, :]
        blk = blk.transpose(0, 1, 3, 2, 4).reshape(bsz, n_cmp, g, CMP_BLOCK * dh)
        return jax.nn.gelu(blk @ w1) @ w2

    kc = compress(k_cmp, ck_w1, ck_w2)
    vc = compress(v_cmp, cv_w1, cv_w2)
    s_c = jnp.einsum('bsgjd,bngd->bgjsn', q, kc).astype(f32) * scale
    mask_c = pos[:, None] >= jnp.asarray(cidx[:, -1])[None, :]
    p_c = masked_softmax(s_c, mask_c)
    o_cmp = jnp.einsum('bgjsn,bngd->bsgjd', p_c.astype(vc.dtype), vc)

    n_sel = s // SEL_BLOCK
    topk = min(SEL_TOPK, n_sel)
    cs = np.arange(n_cmp)[:, None] * CMP_STRIDE
    ss = np.arange(n_sel)[None, :] * SEL_BLOCK
    overlap = np.clip(np.minimum(cs + CMP_BLOCK, ss + SEL_BLOCK) - np.maximum(cs, ss), 0, None) / CMP_BLOCK
    imp = jnp.einsum('bgjsn,nm->bgsm', p_c, jnp.asarray(overlap, f32))
    cur = pos // SEL_BLOCK
    blk_ids = jnp.arange(n_sel)
    forced = (blk_ids[None, :] == cur[:, None]) | (blk_ids[None, :] == 0)
    future = blk_ids[None, :] > cur[:, None]
    imp = jnp.where(forced, FORCE_SCORE, jnp.where(future, NEG_INF, imp))
    _, sel_idx = lax.top_k(imp, topk)
    ks_blocks = k_slc.reshape(bsz, n_sel, SEL_BLOCK, g, dh).transpose(0, 3, 1, 2, 4)
    vs_blocks = v_slc.reshape(bsz, n_sel, SEL_BLOCK, g, dh).transpose(0, 3, 1, 2, 4)
    gather_blocks = jax.vmap(jax.vmap(lambda kb, ib: kb[ib]))
    nqb = s // SEL_BLOCK
    q_sb = q.reshape(bsz, nqb, SEL_BLOCK, g, j, dh).transpose(1, 0, 2, 3, 4, 5)
    idx_sb = sel_idx.reshape(bsz, g, nqb, SEL_BLOCK, topk).transpose(2, 0, 1, 3, 4)

    def sel_block(args):
        q_blk, idx_blk, i = args
        kg = gather_blocks(ks_blocks, idx_blk)
        vg = gather_blocks(vs_blocks, idx_blk)
        sc = jnp.einsum('bqgjd,bgqkld->bgjqkl', q_blk, kg).astype(f32) * scale
        qpos = i * SEL_BLOCK + jnp.arange(SEL_BLOCK)
        kpos = idx_blk[..., None] * SEL_BLOCK + jnp.arange(SEL_BLOCK)
        mask = (kpos <= qpos[None, None, :, None, None])[:, :, None]
        sc = sc.reshape(bsz, g, j, SEL_BLOCK, topk * SEL_BLOCK)
        mask = mask.reshape(bsz, g, 1, SEL_BLOCK, topk * SEL_BLOCK)
        pr = masked_softmax(sc, mask)
        vg = vg.reshape(bsz, g, SEL_BLOCK, topk * SEL_BLOCK, dh)
        return jnp.einsum('bgjqt,bgqtd->bqgjd', pr.astype(vg.dtype), vg)

    o_slc = lax.map(sel_block, (q_sb, idx_sb, jnp.arange(nqb)))
    o_slc = o_slc.transpose(1, 0, 2, 3, 4, 5).reshape(bsz, s, g, j, dh)

    nwb = s // Q_BLOCK
    span = WINDOW + Q_BLOCK
    kw_pad = jnp.pad(k_win, ((0, 0), (WINDOW, 0), (0, 0), (0, 0)))
    vw_pad = jnp.pad(v_win, ((0, 0), (WINDOW, 0), (0, 0), (0, 0)))
    q_wb = q.reshape(bsz, nwb, Q_BLOCK, g, j, dh).transpose(1, 0, 2, 3, 4, 5)

    def win_block(args):
        q_blk, i = args
        start = i * Q_BLOCK
        kb = lax.dynamic_slice_in_dim(kw_pad, start, span, axis=1)
        vb = lax.dynamic_slice_in_dim(vw_pad, start, span, axis=1)
        sc = jnp.einsum('bqgjd,bkgd->bgjqk', q_blk, kb).astype(f32) * scale
        qpos = start + jnp.arange(Q_BLOCK)
        kpos = start - WINDOW + jnp.arange(span)
        diff = qpos[:, None] - kpos[None, :]
        mask = (diff >= 0) & (diff < WINDOW) & (kpos[None, :] >= 0)
        pr = masked_softmax(sc, mask)
        return jnp.einsum('bgjqk,bkgd->bqgjd', pr.astype(vb.dtype), vb)

    o_win = lax.map(win_block, (q_wb, jnp.arange(nwb)))
    o_win = o_win.transpose(1, 0, 2, 3, 4, 5).reshape(bsz, s, g, j, dh)

    gates = jax.nn.sigmoid(gate_logits.astype(f32)).reshape(bsz, s, g, j, 3)
    o = (gates[..., 0:1] * o_cmp.astype(f32) + gates[..., 1:2] * o_slc.astype(f32)
         + gates[..., 2:3] * o_win.astype(f32))
    return o.reshape(bsz, s, NSA_WIDTH)


def _lru_combine(c1, c2):
    a1, b1 = c1
    a2, b2 = c2
    return a1 * a2, a2 * b1 + b2


def rglru_mixer(gate_in, x_in, conv_w, conv_b, w_r, b_r, w_i, b_i, lam):
    bsz, s, _ = x_in.shape
    f32 = jnp.float32
    xc = causal_dwconv(x_in, conv_w, conv_b)
    xb = xc.reshape(bsz, s, RNN_BLOCKS, RNN_BLOCK_DIM)
    r = jax.nn.sigmoid(jnp.einsum('bshi,hio->bsho', xb, w_r).reshape(bsz, s, RNN_WIDTH).astype(f32) + b_r)
    ig = jax.nn.sigmoid(jnp.einsum('bshi,hio->bsho', xb, w_i).reshape(bsz, s, RNN_WIDTH).astype(f32) + b_i)
    log_a = -RG_C * r * jax.nn.softplus(-lam.astype(f32))
    a = jnp.exp(log_a)
    u = jnp.sqrt(-jnp.expm1(2.0 * log_a)) * (ig * xc.astype(f32))
    _, hs = lax.associative_scan(_lru_combine, (a, u), axis=1)
    return hs * jax.nn.gelu(gate_in.astype(f32))


def setup_inputs(seed: int = 0) -> dict:
    key = jax.random.key(seed)
    ks = iter(jax.random.split(key, 64))
    f32 = jnp.float32
    L = DEPTH

    def nrm(shape, scale):
        return jax.random.normal(next(ks), shape, f32) * scale

    def gain(shape):
        return 1.0 + 0.02 * jax.random.normal(next(ks), shape, f32)

    dt0 = jnp.exp(jax.random.uniform(next(ks), (L, SSD_N_HEADS), f32, math.log(1e-3), math.log(1e-1)))
    a0 = jax.random.uniform(next(ks), (L, RNN_WIDTH), f32, 0.9, 0.999)
    return {
        'x': nrm((BATCH, SEQ, D_MODEL), 1.0),
        'p': nrm((L, BATCH, SEQ, PLE_DIM), 1.0),
        'norm_mix': gain((L, D_MODEL)),
        'norm_ffn': gain((L, D_MODEL)),
        'norm_ple': gain((L, D_MODEL)),
        'w_in': nrm((L, D_MODEL, D_IN), D_MODEL ** -0.5),
        'ssd_conv_w': nrm((L, SSD_CONV, SSD_CONV_DIM), SSD_CONV ** -0.5),
        'ssd_conv_b': nrm((L, SSD_CONV_DIM), 0.02),
        'ssd_dt_bias': dt0 + jnp.log(-jnp.expm1(-dt0)),
        'ssd_a_log': jnp.log(jax.random.uniform(next(ks), (L, SSD_N_HEADS), f32, 1.0, 16.0)),
        'ssd_d': gain((L, SSD_N_HEADS)),
        'ssd_norm': gain((L, SSD_D_INNER)),
        'diff_lq1': nrm((L, DIFF_HEAD_DIM), 0.1),
        'diff_lk1': nrm((L, DIFF_HEAD_DIM), 0.1),
        'diff_lq2': nrm((L, DIFF_HEAD_DIM), 0.1),
        'diff_lk2': nrm((L, DIFF_HEAD_DIM), 0.1),
        'diff_norm': gain((L, DIFF_V_DIM)),
        'nsa_pos_cmp': nrm((L, CMP_BLOCK, NSA_HEAD_DIM), 0.02),
        'nsa_ck_w1': nrm((L, CMP_BLOCK * NSA_HEAD_DIM, CMP_HIDDEN), (CMP_BLOCK * NSA_HEAD_DIM) ** -0.5),
        'nsa_ck_w2': nrm((L, CMP_HIDDEN, NSA_HEAD_DIM), CMP_HIDDEN ** -0.5),
        'nsa_cv_w1': nrm((L, CMP_BLOCK * NSA_HEAD_DIM, CMP_HIDDEN), (CMP_BLOCK * NSA_HEAD_DIM) ** -0.5),
        'nsa_cv_w2': nrm((L, CMP_HIDDEN, NSA_HEAD_DIM), CMP_HIDDEN ** -0.5),
        'rnn_conv_w': nrm((L, RNN_CONV, RNN_WIDTH), RNN_CONV ** -0.5),
        'rnn_conv_b': nrm((L, RNN_WIDTH), 0.02),
        'rnn_w_r': nrm((L, RNN_BLOCKS, RNN_BLOCK_DIM, RNN_BLOCK_DIM), RNN_BLOCK_DIM ** -0.5),
        'rnn_b_r': nrm((L, RNN_WIDTH), 0.02),
        'rnn_w_i': nrm((L, RNN_BLOCKS, RNN_BLOCK_DIM, RNN_BLOCK_DIM), RNN_BLOCK_DIM ** -0.5),
        'rnn_b_i': nrm((L, RNN_WIDTH), 0.02),
        'rnn_lambda': jnp.log(a0) - jnp.log1p(-a0),
        'w_merge_gate': nrm((L, N_BRANCH, D_MODEL, D_MODEL), D_MODEL ** -0.5),
        'w_branch': nrm((L, N_BRANCH, MIX_WIDTH, D_MODEL), MIX_WIDTH ** -0.5),
        'w_out': nrm((L, D_MODEL, D_MODEL), D_MODEL ** -0.5),
        'ffn_w_up': nrm((L, D_MODEL, 2 * D_FF), D_MODEL ** -0.5),
        'ffn_conv_w': nrm((L, FFN_CONV, 2 * D_FF), FFN_CONV ** -0.5),
        'ffn_conv_b': nrm((L, 2 * D_FF), 0.02),
        'ffn_w_down': nrm((L, D_FF, D_MODEL), D_FF ** -0.5),
        'ple_w_proj': nrm((L, PLE_DIM, D_MODEL), PLE_DIM ** -0.5),
        'ple_w_gate': nrm((L, D_MODEL, D_MODEL), D_MODEL ** -0.5),
        'norm_final': gain((D_MODEL,)),
    }


def reference(x, p, norm_mix, norm_ffn, norm_ple, w_in,
              ssd_conv_w, ssd_conv_b, ssd_dt_bias, ssd_a_log, ssd_d, ssd_norm,
              diff_lq1, diff_lk1, diff_lq2, diff_lk2, diff_norm,
              nsa_pos_cmp, nsa_ck_w1, nsa_ck_w2, nsa_cv_w1, nsa_cv_w2,
              rnn_conv_w, rnn_conv_b, rnn_w_r, rnn_b_r, rnn_w_i, rnn_b_i, rnn_lambda,
              w_merge_gate, w_branch, w_out,
              ffn_w_up, ffn_conv_w, ffn_conv_b, ffn_w_down,
              ple_w_proj, ple_w_gate, norm_final):
    f32 = jnp.float32
    for l in range(DEPTH):
        h = rmsnorm(x, norm_mix[l])
        proj = h @ w_in[l]
        (a_z, a_xbc, a_dt, b_q, b_k, b_v, c_q, c_kc, c_vc, c_ks, c_vs, c_kw, c_vw, c_g,
         d_gate, d_x) = jnp.split(proj, list(IN_OFFSETS), axis=-1)
        o_a = ssd_mixer(a_z, a_xbc, a_dt, ssd_conv_w[l], ssd_conv_b[l], ssd_dt_bias[l],
                        ssd_a_log[l], ssd_d[l], ssd_norm[l])
        o_b = diff_attention(b_q, b_k, b_v, diff_lq1[l], diff_lk1[l], diff_lq2[l], diff_lk2[l],
                             diff_norm[l], 0.8 - 0.6 * math.exp(-0.3 * l))
        o_c = nsa_attention(c_q, c_kc, c_vc, c_ks, c_vs, c_kw, c_vw, c_g, nsa_pos_cmp[l],
                            nsa_ck_w1[l], nsa_ck_w2[l], nsa_cv_w1[l], nsa_cv_w2[l])
        o_d = rglru_mixer(d_gate, d_x, rnn_conv_w[l], rnn_conv_b[l], rnn_w_r[l], rnn_b_r[l],
                          rnn_w_i[l], rnn_b_i[l], rnn_lambda[l])
        merged = jnp.zeros(x.shape, f32)
        for n, o in enumerate((o_a, o_b, o_c, o_d)):
            gate = jax.nn.sigmoid((h @ w_merge_gate[l, n]).astype(f32))
            merged = merged + gate * (o.astype(x.dtype) @ w_branch[l, n]).astype(f32)
        x = x + (merged.astype(x.dtype) @ w_out[l]).astype(x.dtype)
        h = rmsnorm(x, norm_ffn[l])
        u = causal_dwconv(h @ ffn_w_up[l], ffn_conv_w[l], ffn_conv_b[l])
        u_gate, u_val = jnp.split(u, 2, axis=-1)
        x = x + ((jax.nn.gelu(u_gate) * u_val) @ ffn_w_down[l]).astype(x.dtype)
        g_ple = jax.nn.sigmoid((rmsnorm(x, norm_ple[l]) @ ple_w_gate[l]).astype(f32))
        x = x + (g_ple * (p[l] @ ple_w_proj[l]).astype(f32)).astype(x.dtype)
    return rmsnorm(x, norm_final)
```

```python
import functools
import math

import numpy as np
import jax
import jax.numpy as jnp
from jax import lax
from jax.experimental import pallas as pl
from jax.experimental.pallas import tpu as pltpu

F32 = jnp.float32
BF16 = jnp.bfloat16

D_MODEL = 2048
NORM_EPS = 1e-6
ROPE_THETA = 10000.0
NEG_INF = -1e30
REMOVED = -3e38
FORCE_SCORE = 1e4
LANES = 128
HEAD = 64

MIX_WIDTH = 1024
SSD_N_HEADS = 16
SSD_N_GROUPS = 2
SSD_D_STATE = 128
SSD_CHUNK = 128
SSD_CONV = 4
SSD_CONV_DIM = 1536

DIFF_N_HEADS = 8

NSA_N_KV = 4
CMP_BLOCK = 32
CMP_STRIDE = 16
CMP_HIDDEN = 256
SEL_BLOCK = 64
SEL_SHIFT = 6
SEL_TOPK = 8
WINDOW = 512

RNN_BLOCKS = 16
RNN_CONV = 4
RG_C = 8.0

D_FF = 3 * D_MODEL
PLE_DIM = 256

IN_WIDTHS = (1024, 1536, 16, 1024, 1024, 1024, 1024, 256, 256, 256, 256, 256, 256, 48, 1024, 1024)
IN_OFFSETS = tuple(int(sum(IN_WIDTHS[:i + 1])) for i in range(len(IN_WIDTHS) - 1))

OFF_D_GATE = 0
OFF_D_X = 1024
OFF_A_Z = 2048
OFF_B_Q = 3072
OFF_B_K = 4096
OFF_B_V = 5120
OFF_C_Q = 6144
OFF_A_XS = 7168
OFF_A_BC = 8192
OFF_C_KS = 8704
OFF_C_VS = 8960
OFF_C_KW = 9216
OFF_C_VW = 9472
OFF_C_KC = 9728
OFF_C_VC = 9984
OFF_A_DT = 10240
OFF_C_G = 10368
PROJ_COLS = 10752
PROJ_TN = 512
ROPE_SEGMENTS = ((OFF_B_Q, 2048), (OFF_C_Q, 1024), (OFF_C_KS, 256), (OFF_C_KW, 256), (OFF_C_KC, 256))

VMEM_LIMIT = 56 * 1024 * 1024


def _params(sem, vmem=VMEM_LIMIT):
    return pltpu.CompilerParams(dimension_semantics=sem, vmem_limit_bytes=vmem)


def _dot(a, b):
    return jnp.dot(a, b, preferred_element_type=F32)


def _dot_nt(a, b):
    return lax.dot_general(a, b, (((1,), (1,)), ((), ())), preferred_element_type=F32)


def _split_bf16(x, terms):
    parts = []
    r = x
    for _ in range(terms):
        p = r.astype(BF16)
        parts.append(p)
        r = r - p.astype(F32)
    return parts


def _dot_split_lhs(x, e, terms=3):
    acc = None
    for p in _split_bf16(x, terms):
        d = _dot(p, e)
        acc = d if acc is None else acc + d
    return acc


def _dot_split_rhs(e, x, terms=3):
    acc = None
    for p in _split_bf16(x, terms):
        d = _dot(e, p)
        acc = d if acc is None else acc + d
    return acc


def _sigmoid(x):
    return jax.nn.sigmoid(x)


def _silu(x):
    return x * _sigmoid(x)


def _gelu(x):
    c = math.sqrt(2.0 / math.pi)
    return 0.5 * x * (1.0 + jnp.tanh(c * (x + 0.044715 * (x * x * x))))


def _softplus(x):
    return jnp.maximum(x, 0.0) + jnp.log1p(jnp.exp(-jnp.abs(x)))


def _rms(x, g):
    return x * lax.rsqrt(jnp.mean(x * x, axis=-1, keepdims=True) + NORM_EPS) * g


def _rmsnorm_kernel(x_ref, g_ref, o_ref):
    o_ref[...] = _rms(x_ref[...], g_ref[...]).astype(o_ref.dtype)


def _rmsnorm(x, g, out_dtype, tm=512):
    m, d = x.shape
    return pl.pallas_call(
        _rmsnorm_kernel,
        grid=(m // tm,),
        in_specs=[pl.BlockSpec((tm, d), lambda i: (i, 0)), pl.BlockSpec((1, d), lambda i: (0, 0))],
        out_specs=pl.BlockSpec((tm, d), lambda i: (i, 0)),
        out_shape=jax.ShapeDtypeStruct((m, d), out_dtype),
        compiler_params=_params(("arbitrary",)),
        name="rmsnorm",
    )(x, g.reshape(1, d))


def _proj_kernel(flags_ref, h_ref, w_ref, cos_ref, sin_ref, o_ref):
    j = pl.program_id(1)
    tm, tn = o_ref.shape
    nchunk = tn // LANES
    acc = _dot(h_ref[...], w_ref[...])
    lane = lax.broadcasted_iota(jnp.int32, (tm, LANES), 1)
    first_half = (lane & (HEAD - 1)) < (HEAD // 2)
    for c in range(nchunk):
        blk = acc[:, c * LANES:(c + 1) * LANES]
        flag = flags_ref[j * nchunk + c]

        @pl.when(flag != 0)
        def _():
            partner = jnp.where(first_half, pltpu.roll(blk, LANES - HEAD // 2, 1), pltpu.roll(blk, HEAD // 2, 1))
            o_ref[:, c * LANES:(c + 1) * LANES] = blk * cos_ref[...] + partner * sin_ref[...]

        @pl.when(flag == 0)
        def _():
            o_ref[:, c * LANES:(c + 1) * LANES] = blk


def _project(h, w_perm, flags, cos_t, sin_t, seq, tm=1024):
    m, k = h.shape
    n = w_perm.shape[1]
    tn = PROJ_TN
    pos_blocks = seq // tm
    grid_spec = pltpu.PrefetchScalarGridSpec(
        num_scalar_prefetch=1,
        grid=(m // tm, n // tn),
        in_specs=[
            pl.BlockSpec((tm, k), lambda i, j, f: (i, 0)),
            pl.BlockSpec((k, tn), lambda i, j, f: (0, j)),
            pl.BlockSpec((tm, LANES), lambda i, j, f: (i % pos_blocks, 0)),
            pl.BlockSpec((tm, LANES), lambda i, j, f: (i % pos_blocks, 0)),
        ],
        out_specs=pl.BlockSpec((tm, tn), lambda i, j, f: (i, j)),
    )
    return pl.pallas_call(
        _proj_kernel,
        grid_spec=grid_spec,
        out_shape=jax.ShapeDtypeStruct((m, n), F32),
        compiler_params=_params(("arbitrary", "arbitrary")),
        name="in_proj",
    )(flags, h, w_perm, cos_t, sin_t)


def _ssd_kernel(z_ref, xs_ref, bc_ref, dt_ref, cw_ref, cb_ref, dtb_ref, alog_ref, dexp_ref, gn_ref,
                e64_ref, e128_ref, o_ref, cbuf, ht, ybuf):
    c = pl.program_id(1)
    cl = SSD_CHUNK
    nx = MIX_WIDTH

    @pl.when(c == 0)
    def _():
        cbuf[0:8, :] = jnp.zeros((8, SSD_CONV_DIM), F32)
        ht[...] = jnp.zeros(ht.shape, F32)

    cbuf[8:8 + cl, 0:nx] = xs_ref[...]
    cbuf[8:8 + cl, nx:SSD_CONV_DIM] = bc_ref[...]
    conv = cb_ref[...]
    for k in range(SSD_CONV):
        conv = conv + cbuf[5 + k:5 + k + cl, :] * cw_ref[k:k + 1, :]
    cbuf[0:8, :] = cbuf[cl:cl + 8, :]
    xbc = _silu(conv)
    xs = xbc[:, 0:nx]

    dt = _softplus(dt_ref[...] + dtb_ref[...])
    a = -jnp.exp(alog_ref[...])
    a_dt = dt * a
    li = lax.broadcasted_iota(jnp.int32, (cl, cl), 0)
    si = lax.broadcasted_iota(jnp.int32, (cl, cl), 1)
    causal = li >= si
    tril = jnp.where(causal, 1.0, 0.0).astype(BF16)
    acs = _dot_split_rhs(tril, a_dt)
    acs_t = acs.T
    e64 = e64_ref[...]
    dtx = _dot_split_lhs(dt, e64)
    acsx = _dot_split_lhs(acs, e64)
    acol = _dot_split_lhs(acs, e128_ref[...])
    xdt = xs * dtx
    xdt_b = xdt.astype(BF16)
    last = acsx[cl - 1:cl, :]
    decay_out = jnp.exp(acsx)
    xw_b = (xdt * jnp.exp(last - acsx)).astype(BF16)
    d_a = jnp.exp(last)
    lane = lax.broadcasted_iota(jnp.int32, (cl, LANES), 1)
    lo = lane < HEAD
    pairs_per_group = SSD_N_HEADS // SSD_N_GROUPS // 2
    for g in range(SSD_N_GROUPS):
        b_g = xbc[:, nx + g * SSD_D_STATE:nx + (g + 1) * SSD_D_STATE]
        c_g = xbc[:, nx + (SSD_N_GROUPS + g) * SSD_D_STATE:nx + (SSD_N_GROUPS + g + 1) * SSD_D_STATE]
        c_gb = c_g.astype(BF16)
        cb = _dot_nt(c_gb, b_g.astype(BF16))
        b_gt = b_g.T.astype(BF16)
        for pi in range(pairs_per_group):
            p = g * pairs_per_group + pi
            sl = slice(p * LANES, (p + 1) * LANES)
            ms = []
            for h in (2 * p, 2 * p + 1):
                seg = acol[:, h * LANES:(h + 1) * LANES] - acs_t[h:h + 1, :]
                ms.append((cb * jnp.where(causal, jnp.exp(seg), 0.0)).astype(BF16))
            x_p = xdt_b[:, sl]
            y_diag = jnp.where(lo, _dot(ms[0], x_p), _dot(ms[1], x_p))
            h_p = ht[p]
            y_off = _dot(c_gb, h_p.astype(BF16)) * decay_out[:, sl]
            ht[p] = h_p * d_a[:, sl] + _dot(b_gt, xw_b[:, sl])
            ybuf[:, sl] = y_diag + y_off + xs[:, sl] * dexp_ref[:, sl]
    yz = ybuf[...] * _silu(z_ref[...])
    o_ref[...] = _rms(yz, gn_ref[...]).astype(o_ref.dtype)


def _ssd(proj, bsz, seq, conv_w, conv_b, dt_bias, a_log, d_skip, g_norm):
    m = proj.shape[0]
    cl = SSD_CHUNK
    nc = seq // cl
    pad = LANES - SSD_N_HEADS
    dtb = jnp.pad(dt_bias, (0, pad)).reshape(1, LANES)
    alog = jnp.pad(a_log, (0, pad)).reshape(1, LANES)
    dexp = jnp.repeat(d_skip, HEAD).reshape(1, MIX_WIDTH)
    hidx = np.arange(LANES)[:, None]
    e64 = jnp.asarray((np.arange(MIX_WIDTH)[None, :] // HEAD) == hidx, BF16)
    e128 = jnp.asarray((np.arange(SSD_N_HEADS * LANES)[None, :] // LANES) == hidx, BF16)
    row = lambda b, c: b * nc + c
    const = lambda b, c: (0, 0)
    return pl.pallas_call(
        _ssd_kernel,
        grid=(bsz, nc),
        in_specs=[
            pl.BlockSpec((cl, 1024), lambda b, c: (row(b, c), OFF_A_Z // 1024)),
            pl.BlockSpec((cl, 1024), lambda b, c: (row(b, c), OFF_A_XS // 1024)),
            pl.BlockSpec((cl, 512), lambda b, c: (row(b, c), OFF_A_BC // 512)),
            pl.BlockSpec((cl, LANES), lambda b, c: (row(b, c), OFF_A_DT // LANES)),
            pl.BlockSpec((SSD_CONV, SSD_CONV_DIM), const),
            pl.BlockSpec((1, SSD_CONV_DIM), const),
            pl.BlockSpec((1, LANES), const),
            pl.BlockSpec((1, LANES), const),
            pl.BlockSpec((1, MIX_WIDTH), const),
            pl.BlockSpec((1, MIX_WIDTH), const),
            pl.BlockSpec((LANES, MIX_WIDTH), const),
            pl.BlockSpec((LANES, SSD_N_HEADS * LANES), const),
        ],
        out_specs=pl.BlockSpec((cl, MIX_WIDTH), lambda b, c: (row(b, c), 0)),
        out_shape=jax.ShapeDtypeStruct((m, MIX_WIDTH), BF16),
        scratch_shapes=[
            pltpu.VMEM((cl + 8, SSD_CONV_DIM), F32),
            pltpu.VMEM((SSD_N_HEADS // 2, SSD_D_STATE, LANES), F32),
            pltpu.VMEM((cl, MIX_WIDTH), F32),
        ],
        compiler_params=_params(("arbitrary", "arbitrary")),
        name="ssd_mixer",
    )(proj, proj, proj, proj, conv_w, conv_b.reshape(1, -1), dtb, alog, dexp, g_norm.reshape(1, -1), e64, e128)


def _rglru_kernel(gate_ref, x_ref, cw_ref, cb_ref, wri_ref, br_ref, bi_ref, lam_ref, o_ref,
                  cbuf, sa, su, hprev):
    c = pl.program_id(1)
    t, w = x_ref.shape

    @pl.when(c == 0)
    def _():
        cbuf[0:8, :] = jnp.zeros((8, w), F32)
        hprev[...] = jnp.zeros(hprev.shape, F32)

    cbuf[8:8 + t, :] = x_ref[...]
    xc = cb_ref[...]
    for k in range(RNN_CONV):
        xc = xc + cbuf[5 + k:5 + k + t, :] * cw_ref[k:k + 1, :]
    cbuf[0:8, :] = cbuf[t:t + 8, :]
    xcb = xc.astype(BF16)
    ngrp = wri_ref.shape[0]
    gw = w // ngrp
    r_parts, i_parts = [], []
    for q in range(ngrp):
        ri = _dot(xcb[:, q * gw:(q + 1) * gw], wri_ref[q])
        r_parts.append(ri[:, 0:gw])
        i_parts.append(ri[:, gw:2 * gw])
    r = _sigmoid(jnp.concatenate(r_parts, axis=1) + br_ref[...])
    ig = _sigmoid(jnp.concatenate(i_parts, axis=1) + bi_ref[...])
    log_a = (-RG_C * r) * _softplus(-lam_ref[...])
    a = jnp.exp(log_a)
    u = jnp.sqrt(-jnp.tanh(log_a) * (a * a + 1.0)) * (ig * xc)

    sa[0:8, :] = jnp.ones((8, w), F32)
    su[0:8, :] = jnp.zeros((8, w), F32)
    shift = 1
    while shift < 8:
        sa[8:8 + t, :] = a
        su[8:8 + t, :] = u
        a_s = sa[8 - shift:8 - shift + t, :]
        u_s = su[8 - shift:8 - shift + t, :]
        u = a * u_s + u
        a = a * a_s
        shift *= 2
    while shift < t:
        a_s = jnp.concatenate([jnp.ones((shift, w), F32), a[0:t - shift, :]], axis=0)
        u_s = jnp.concatenate([jnp.zeros((shift, w), F32), u[0:t - shift, :]], axis=0)
        u = a * u_s + u
        a = a * a_s
        shift *= 2
    h = u + a * hprev[...]
    hprev[...] = h[t - 1:t, :]
    o_ref[...] = (h * _gelu(gate_ref[...])).astype(o_ref.dtype)


def _rglru(proj, bsz, seq, conv_w, conv_b, w_r, b_r, w_i, b_i, lam, t=256):
    m = proj.shape[0]
    w = MIX_WIDTH
    nt = seq // t
    ngrp, per = 4, RNN_BLOCKS // 4
    bd = w_r.shape[-1]
    gw = per * bd

    def blockdiag(wb):
        wg = wb.reshape(ngrp, per, bd, bd)
        eye = jnp.eye(per, dtype=wb.dtype)
        return jnp.einsum('gpio,pq->gpiqo', wg, eye).reshape(ngrp, gw, gw)

    wri = jnp.concatenate([blockdiag(w_r), blockdiag(w_i)], axis=-1).astype(BF16)
    row = lambda b, c: b * nt + c
    const2 = lambda b, c: (0, 0)
    return pl.pallas_call(
        _rglru_kernel,
        grid=(bsz, nt),
        in_specs=[
            pl.BlockSpec((t, w), lambda b, c: (row(b, c), OFF_D_GATE // 1024)),
            pl.BlockSpec((t, w), lambda b, c: (row(b, c), OFF_D_X // 1024)),
            pl.BlockSpec((RNN_CONV, w), const2),
            pl.BlockSpec((1, w), const2),
            pl.BlockSpec((ngrp, gw, 2 * gw), lambda b, c: (0, 0, 0)),
            pl.BlockSpec((1, w), const2),
            pl.BlockSpec((1, w), const2),
            pl.BlockSpec((1, w), const2),
        ],
        out_specs=pl.BlockSpec((t, w), lambda b, c: (row(b, c), 0)),
        out_shape=jax.ShapeDtypeStruct((m, w), BF16),
        scratch_shapes=[
            pltpu.VMEM((t + 8, w), F32),
            pltpu.VMEM((t + 8, w), F32),
            pltpu.VMEM((t + 8, w), F32),
            pltpu.VMEM((1, w), F32),
        ],
        compiler_params=_params(("arbitrary", "arbitrary")),
        name="rglru_mixer",
    )(proj, proj, conv_w, conv_b.reshape(1, w), wri, b_r.reshape(1, w), b_i.reshape(1, w), lam.reshape(1, w))


def _online_update(s, ok, m_old, l_old):
    if ok is not None:
        s = jnp.where(ok, s, NEG_INF)
    m_new = jnp.maximum(m_old, jnp.max(s, axis=-1, keepdims=True))
    alpha = jnp.exp(m_old - m_new)
    p = jnp.exp(s - m_new)
    if ok is not None:
        p = jnp.where(ok, p, 0.0)
    l_new = alpha * l_old + jnp.sum(p, axis=-1, keepdims=True)
    return p, m_new, l_new, alpha


def _diff_kernel(q_ref, k_ref, v_ref, lq1_ref, lk1_ref, lq2_ref, lk2_ref, gn_ref, o_ref,
                 k0, k1, vb, *, lam_init, tk):
    i = pl.program_id(2)
    tq = q_ref.shape[0]
    seq = k_ref.shape[0]

    @pl.when(i == 0)
    def _():
        lo = lax.broadcasted_iota(jnp.int32, (seq, LANES), 1) < HEAD
        k = k_ref[...]
        k0[...] = jnp.where(lo, k, 0.0).astype(BF16)
        k1[...] = jnp.where(lo, 0.0, k).astype(BF16)
        vb[...] = v_ref[...].astype(BF16)

    q = q_ref[...].astype(BF16)
    row = i * tq + lax.broadcasted_iota(jnp.int32, (tq, tk), 0)
    col = lax.broadcasted_iota(jnp.int32, (tq, tk), 1)

    def step(kt, carry, masked):
        m1, l1, a1, m2, l2, a2 = carry
        sl = pl.ds(pl.multiple_of(kt * tk, tk), tk)
        v = vb[sl, :]
        ok = ((kt * tk + col) <= row) if masked else None
        p1, m1, l1, al1 = _online_update(_dot_nt(q, k0[sl, :]), ok, m1, l1)
        a1 = al1 * a1 + _dot(p1.astype(BF16), v)
        p2, m2, l2, al2 = _online_update(_dot_nt(q, k1[sl, :]), ok, m2, l2)
        a2 = al2 * a2 + _dot(p2.astype(BF16), v)
        return m1, l1, a1, m2, l2, a2

    neg = jnp.full((tq, 1), NEG_INF, F32)
    zero1 = jnp.zeros((tq, 1), F32)
    zacc = jnp.zeros((tq, LANES), F32)
    per = tq // tk
    carry = (neg, zero1, zacc, neg, zero1, zacc)
    carry = lax.fori_loop(0, i * per, lambda kt, cr: step(kt, cr, False), carry)
    for d in range(per):
        carry = step(i * per + d, carry, True)
    m1, l1, a1, m2, l2, a2 = carry
    lam = (jnp.exp(jnp.sum(lq1_ref[...] * lk1_ref[...], axis=-1, keepdims=True))
           - jnp.exp(jnp.sum(lq2_ref[...] * lk2_ref[...], axis=-1, keepdims=True)) + lam_init)
    o = a1 / l1 - lam * (a2 / l2)
    o_ref[...] = (_rms(o, gn_ref[...]) * (1.0 - lam_init)).astype(o_ref.dtype)


def _diff_attention(proj, bsz, seq, lq1, lk1, lq2, lk2, g_norm, lam_init, tq=256, tk=256):
    m = proj.shape[0]
    nq = seq // tq
    padv = lambda v: jnp.pad(v, (0, LANES - v.shape[0])).reshape(1, LANES)
    const = lambda b, h, i: (0, 0)
    return pl.pallas_call(
        functools.partial(_diff_kernel, lam_init=lam_init, tk=tk),
        grid=(bsz, DIFF_N_HEADS, nq),
        in_specs=[
            pl.BlockSpec((tq, LANES), lambda b, h, i: (b * nq + i, OFF_B_Q // LANES + h)),
            pl.BlockSpec((seq, LANES), lambda b, h, i: (b, OFF_B_K // LANES + h)),
            pl.BlockSpec((seq, LANES), lambda b, h, i: (b, OFF_B_V // LANES + h)),
            pl.BlockSpec((1, LANES), const),
            pl.BlockSpec((1, LANES), const),
            pl.BlockSpec((1, LANES), const),
            pl.BlockSpec((1, LANES), const),
            pl.BlockSpec((1, LANES), const),
        ],
        out_specs=pl.BlockSpec((tq, LANES), lambda b, h, i: (b * nq + i, h)),
        out_shape=jax.ShapeDtypeStruct((m, MIX_WIDTH), BF16),
        scratch_shapes=[pltpu.VMEM((seq, LANES), BF16)] * 3,
        compiler_params=_params(("arbitrary", "arbitrary", "arbitrary")),
        name="diff_attention",
    )(proj, proj, proj, padv(lq1), padv(lk1), padv(lq2), padv(lk2), g_norm.reshape(1, LANES))


def _compress_kernel(x0_ref, x1_ref, wa_ref, wb_ref, pa_ref, pb_ref, w2_ref, o_ref, acc_a, acc_b, sbuf):
    nrow = o_ref.shape[0]
    x_refs = (x0_ref, x1_ref)
    hid2 = acc_a.shape[1] // len(x_refs)
    for l in range(CMP_STRIDE):
        for pr, x_ref in enumerate(x_refs):
            cs = slice(pr * hid2, (pr + 1) * hid2)
            ls = slice(pr * LANES, (pr + 1) * LANES)
            x = x_ref[pl.ds(l, nrow, stride=CMP_STRIDE), :]
            da = _dot((x + pa_ref[l][:, ls]).astype(BF16), wa_ref[l, pr])
            db = _dot((x + pb_ref[l][:, ls]).astype(BF16), wb_ref[l, pr])
            if l == 0:
                acc_a[:, cs] = da
                acc_b[:, cs] = db
            else:
                acc_a[:, cs] += da
                acc_b[:, cs] += db
    sbuf[0:nrow, :] = acc_b[...]
    sbuf[nrow:nrow + 8, :] = jnp.zeros((8, sbuf.shape[1]), F32)
    pre = acc_a[...] + sbuf[1:nrow + 1, :]
    o_ref[...] = _dot(_gelu(pre).astype(BF16), w2_ref[...])


def _compress(proj, bsz, seq, pos_cmp, k_w1, k_w2, v_w1, v_w2, batches_per_step=2):
    g, dh, hid = NSA_N_KV, HEAD, CMP_HIDDEN
    width = g * dh
    chunks = seq // CMP_STRIDE
    nrow = batches_per_step * chunks
    nstep = bsz // batches_per_step

    def expand_w1(w1):
        w = w1.reshape(2, CMP_STRIDE, dh, hid)
        eye = jnp.eye(2, dtype=w1.dtype)
        e = jnp.einsum('hldk,ab->hladbk', w, eye).reshape(2, CMP_STRIDE, 2 * dh, 2 * hid)
        e = jnp.broadcast_to(e[:, :, None], (2, CMP_STRIDE, g // 2, 2 * dh, 2 * hid))
        return e[0].astype(BF16), e[1].astype(BF16)

    def expand_w2(w2):
        eye = jnp.eye(g, dtype=w2.dtype)
        return jnp.einsum('kd,ab->akbd', w2, eye).reshape(g * hid, g * dh).astype(BF16)

    ka, kb = expand_w1(k_w1)
    va, vb = expand_w1(v_w1)
    wa = jnp.stack([ka, va])
    wb = jnp.stack([kb, vb])
    w2 = jnp.stack([expand_w2(k_w2), expand_w2(v_w2)])
    pos = jnp.tile(pos_cmp, (1, g)).reshape(2, CMP_STRIDE, 1, width)
    return pl.pallas_call(
        _compress_kernel,
        grid=(2, nstep),
        in_specs=[
            pl.BlockSpec((batches_per_step * seq, LANES), lambda t, s: (s, OFF_C_KC // LANES + 2 * t)),
            pl.BlockSpec((batches_per_step * seq, LANES), lambda t, s: (s, OFF_C_KC // LANES + 2 * t + 1)),
            pl.BlockSpec((None, CMP_STRIDE, g // 2, 2 * dh, 2 * hid), lambda t, s: (t, 0, 0, 0, 0)),
            pl.BlockSpec((None, CMP_STRIDE, g // 2, 2 * dh, 2 * hid), lambda t, s: (t, 0, 0, 0, 0)),
            pl.BlockSpec((None, CMP_STRIDE, 1, width), lambda t, s: (0, 0, 0, 0)),
            pl.BlockSpec((None, CMP_STRIDE, 1, width), lambda t, s: (1, 0, 0, 0)),
            pl.BlockSpec((None, g * hid, width), lambda t, s: (t, 0, 0)),
        ],
        out_specs=pl.BlockSpec((None, nrow, width), lambda t, s: (t, s, 0)),
        out_shape=jax.ShapeDtypeStruct((2, bsz * chunks, width), F32),
        scratch_shapes=[
            pltpu.VMEM((nrow, g * hid), F32),
            pltpu.VMEM((nrow, g * hid), F32),
            pltpu.VMEM((nrow + 8, g * hid), F32),
        ],
        compiler_params=_params(("arbitrary", "arbitrary")),
        name="nsa_compress",
    )(proj, proj, wa, wb, pos, pos, w2)


def _nsa_kernel(q_ref, ks_ref, vs_ref, kw_ref, vw_ref, kc_ref, vc_ref, gate_ref, ovt_ref, eg_ref, o_ref,
                ks0, ks1, vs0, vs1, kw0, kw1, vw0, vw1, kc0, kc1, vc0, vc1, *, tk):
    g = pl.program_id(1)
    i = pl.program_id(2)
    tq = q_ref.shape[0]
    odd = (g % 2) == 1

    @pl.when(i == 0)
    def _():
        def prep(src, d0, d1):
            x = src[...]
            lo = lax.broadcasted_iota(jnp.int32, x.shape, 1) < HEAD
            xr = pltpu.roll(x, HEAD, 1)
            d0[...] = jnp.where(lo, jnp.where(odd, xr, x), 0.0).astype(BF16)
            d1[...] = jnp.where(lo, 0.0, jnp.where(odd, x, xr)).astype(BF16)
        prep(ks_ref, ks0, ks1)
        prep(vs_ref, vs0, vs1)
        prep(kw_ref, kw0, kw1)
        prep(vw_ref, vw0, vw1)
        prep(kc_ref, kc0, kc1)
        prep(vc_ref, vc0, vc1)

    q = q_ref[...].astype(BF16)
    qp = (q[:, 0:LANES], q[:, LANES:2 * LANES])
    lane = lax.broadcasted_iota(jnp.int32, (tq, LANES), 1)
    lo = lane < HEAD
    trow = i * tq + lax.broadcasted_iota(jnp.int32, (tq, LANES), 0)

    cmask = (CMP_STRIDE * lane + (CMP_BLOCK - 1)) <= trow
    psum = jnp.zeros((tq, LANES), F32)
    o_cmp = []
    for pp in range(2):
        oc = jnp.zeros((tq, LANES), F32)
        for hh, (kz, vz) in enumerate(((kc0, vc0), (kc1, vc1))):
            s = jnp.where(cmask, _dot_nt(qp[pp], kz[...]), NEG_INF)
            mx = jnp.max(s, axis=-1, keepdims=True)
            e = jnp.where(cmask, jnp.exp(s - mx), 0.0)
            p = e / jnp.maximum(jnp.sum(e, axis=-1, keepdims=True), 1e-30)
            psum = psum + p
            oc = oc + _dot(p.astype(BF16), vz[...])
        o_cmp.append(oc)

    ovt = ovt_ref[...]
    n_sel = ovt.shape[0]
    imp = None
    for part in _split_bf16(psum, 3):
        d = _dot_nt(ovt, part)
        imp = d if imp is None else imp + d
    m_idx = lax.broadcasted_iota(jnp.int32, (n_sel, tq), 0)
    cur = (i * tq + lax.broadcasted_iota(jnp.int32, (n_sel, tq), 1)) >> SEL_SHIFT
    val = jnp.where((m_idx == cur) | (m_idx == 0), FORCE_SCORE, jnp.where(m_idx > cur, NEG_INF, imp))
    mem_t = jnp.zeros((n_sel, tq), F32)
    for _ in range(min(SEL_TOPK, n_sel)):
        mx = jnp.max(val, axis=0, keepdims=True)
        pick = jnp.min(jnp.where(val == mx, m_idx, n_sel), axis=0, keepdims=True)
        hit = m_idx == pick
        mem_t = jnp.where(hit, 1.0, mem_t)
        val = jnp.where(hit, REMOVED, val)
    mem = jnp.concatenate([mem_t, jnp.zeros((LANES - n_sel, tq), F32)], axis=0).T.astype(BF16)

    qrow = i * tq + lax.broadcasted_iota(jnp.int32, (tq, tk), 0)
    kcol = lax.broadcasted_iota(jnp.int32, (tq, tk), 1)

    def attend(kt, carry, kz_refs, vz_refs, ok):
        ms, ls, accs = carry
        sl = pl.ds(pl.multiple_of(kt * tk, tk), tk)
        ms_n, ls_n, accs_n = [], [], []
        for pp in range(2):
            acc = accs[pp]
            upd = jnp.zeros((tq, LANES), F32)
            alphas = []
            for hh in range(2):
                hd = 2 * pp + hh
                p, m_new, l_new, alpha = _online_update(_dot_nt(qp[pp], kz_refs[hh][sl, :]), ok, ms[hd], ls[hd])
                upd = upd + _dot(p.astype(BF16), vz_refs[hh][sl, :])
                ms_n.append(m_new)
                ls_n.append(l_new)
                alphas.append(alpha)
            accs_n.append(acc * jnp.where(lo, alphas[0], alphas[1]) + upd)
        return tuple(ms_n), tuple(ls_n), tuple(accs_n)

    def finish(carry):
        _, ls, accs = carry
        return [accs[pp] / jnp.where(lo, ls[2 * pp], ls[2 * pp + 1]) for pp in range(2)]

    neg = jnp.full((tq, 1), NEG_INF, F32)
    zero1 = jnp.zeros((tq, 1), F32)
    zacc = jnp.zeros((tq, LANES), F32)
    init = ((neg,) * 4, (zero1,) * 4, (zacc,) * 2)
    per = tq // tk

    blk_row = lax.broadcasted_iota(jnp.int32, (LANES, tk), 0)
    blk_col = lax.broadcasted_iota(jnp.int32, (LANES, tk), 1)

    def sel_step(kt, carry):
        expand = jnp.where(blk_row == ((kt * tk + blk_col) >> SEL_SHIFT), 1.0, 0.0).astype(BF16)
        ok = (_dot(mem, expand) > 0.5) & ((kt * tk + kcol) <= qrow)
        return attend(kt, carry, (ks0, ks1), (vs0, vs1), ok)

    o_slc = finish(lax.fori_loop(0, (i + 1) * per, sel_step, init))

    def win_step(kt, carry):
        diff = qrow - (kt * tk + kcol)
        ok = (diff >= 0) & (diff < WINDOW)
        return attend(kt, carry, (kw0, kw1), (vw0, vw1), ok)

    first = jnp.maximum(i * tq - WINDOW + 1, 0) // tk
    o_win = finish(lax.fori_loop(first, (i + 1) * per, win_step, init))

    gexp = _dot_split_lhs(_sigmoid(gate_ref[...]), eg_ref[...])
    width = 2 * LANES
    cat = lambda ps: jnp.concatenate(ps, axis=1)
    o = (gexp[:, 0:width] * cat(o_cmp) + gexp[:, width:2 * width] * cat(o_slc)
         + gexp[:, 2 * width:3 * width] * cat(o_win))
    o_ref[...] = o.astype(o_ref.dtype)


def _nsa_attention(proj, cmp_kv, bsz, seq, tq=256, tk=256):
    m = proj.shape[0]
    nq = seq // tq
    g = NSA_N_KV
    hpg = MIX_WIDTH // HEAD // g
    n_cmp = (seq - CMP_BLOCK) // CMP_STRIDE + 1
    n_sel = seq // SEL_BLOCK
    chunks = seq // CMP_STRIDE
    cs = np.arange(n_cmp)[:, None] * CMP_STRIDE
    ss = np.arange(n_sel)[None, :] * SEL_BLOCK
    overlap = np.clip(np.minimum(cs + CMP_BLOCK, ss + SEL_BLOCK) - np.maximum(cs, ss), 0, None) / CMP_BLOCK
    ovt = np.zeros((n_sel, chunks), np.float32)
    ovt[:, :n_cmp] = overlap.T
    eg = np.zeros((g, LANES, 3 * hpg * HEAD), np.float32)
    for ty in range(3):
        for hd in range(g * hpg):
            gg, j = divmod(hd, hpg)
            eg[gg, ty * g * hpg + hd, ty * hpg * HEAD + j * HEAD: ty * hpg * HEAD + (j + 1) * HEAD] = 1.0
    kvspec = lambda off: pl.BlockSpec((seq, LANES), lambda b, gi, i: (b, off // LANES + gi // 2))
    cmpspec = lambda t: pl.BlockSpec((None, chunks, LANES), lambda b, gi, i: (t, b, gi // 2))
    return pl.pallas_call(
        functools.partial(_nsa_kernel, tk=tk),
        grid=(bsz, g, nq),
        in_specs=[
            pl.BlockSpec((tq, hpg * HEAD), lambda b, gi, i: (b * nq + i, OFF_C_Q // (hpg * HEAD) + gi)),
            kvspec(OFF_C_KS), kvspec(OFF_C_VS), kvspec(OFF_C_KW), kvspec(OFF_C_VW),
            cmpspec(0), cmpspec(1),
            pl.BlockSpec((tq, LANES), lambda b, gi, i: (b * nq + i, OFF_C_G // LANES)),
            pl.BlockSpec((n_sel, chunks), lambda b, gi, i: (0, 0)),
            pl.BlockSpec((None, LANES, 3 * hpg * HEAD), lambda b, gi, i: (gi, 0, 0)),
        ],
        out_specs=pl.BlockSpec((tq, hpg * HEAD), lambda b, gi, i: (b * nq + i, gi)),
        out_shape=jax.ShapeDtypeStruct((m, MIX_WIDTH), BF16),
        scratch_shapes=[pltpu.VMEM((seq, LANES), BF16)] * 8 + [pltpu.VMEM((chunks, LANES), BF16)] * 4,
        compiler_params=_params(("arbitrary", "arbitrary", "arbitrary")),
        name="nsa_attention",
    )(proj, proj, proj, proj, proj, cmp_kv, cmp_kv, proj, jnp.asarray(ovt, BF16), jnp.asarray(eg, BF16))


def _merge_kernel(h_ref, oa_ref, ob_ref, oc_ref, od_ref, wg_ref, wb_ref, o_ref):
    h = h_ref[...]
    acc = None
    for n, br in enumerate((oa_ref, ob_ref, oc_ref, od_ref)):
        term = _sigmoid(_dot(h, wg_ref[n])) * _dot(br[...], wb_ref[n])
        acc = term if acc is None else acc + term
    o_ref[...] = acc.astype(o_ref.dtype)


def _merge(h, branches, wg, wb, tm=1024, tn=256):
    m, d = h.shape
    nb, kb, n = wb.shape
    bspec = pl.BlockSpec((tm, kb), lambda i, j: (i, 0))
    return pl.pallas_call(
        _merge_kernel,
        grid=(m // tm, n // tn),
        in_specs=[
            pl.BlockSpec((tm, d), lambda i, j: (i, 0)),
            bspec, bspec, bspec, bspec,
            pl.BlockSpec((nb, d, tn), lambda i, j: (0, 0, j)),
            pl.BlockSpec((nb, kb, tn), lambda i, j: (0, 0, j)),
        ],
        out_specs=pl.BlockSpec((tm, tn), lambda i, j: (i, j)),
        out_shape=jax.ShapeDtypeStruct((m, n), BF16),
        compiler_params=_params(("arbitrary", "arbitrary")),
        name="branch_merge",
    )(h, *branches, wg, wb)


def _wout_kernel(a_ref, w_ref, x_ref, o_ref):
    o_ref[...] = x_ref[...] + _dot(a_ref[...], w_ref[...])


def _out_proj(merged, w, x, tm=512):
    m, k = merged.shape
    n = w.shape[1]
    return pl.pallas_call(
        _wout_kernel,
        grid=(m // tm,),
        in_specs=[
            pl.BlockSpec((tm, k), lambda i: (i, 0)),
            pl.BlockSpec((k, n), lambda i: (0, 0)),
            pl.BlockSpec((tm, n), lambda i: (i, 0)),
        ],
        out_specs=pl.BlockSpec((tm, n), lambda i: (i, 0)),
        out_shape=jax.ShapeDtypeStruct((m, n), F32),
        compiler_params=_params(("arbitrary",)),
        name="out_proj",
    )(merged, w, x)


def _ffn_kernel(x_ref, gn_ref, wug_ref, wuv_ref, cwg_ref, cwv_ref, cbg_ref, cbv_ref, wd_ref, o_ref,
                hbuf, acc, sg, sv, carry, *, blocks_per_seq):
    i = pl.program_id(0)
    f = pl.program_id(1)
    nf = pl.num_programs(1)
    tm = x_ref.shape[0]
    tf = wug_ref.shape[1]
    seq_start = (i % blocks_per_seq) == 0

    @pl.when(f == 0)
    def _():
        hbuf[...] = _rms(x_ref[...], gn_ref[...]).astype(BF16)

    h = hbuf[...]
    ug = _dot(h, wug_ref[...])
    uv = _dot(h, wuv_ref[...])

    @pl.when(seq_start)
    def _():
        sg[0:8, :] = jnp.zeros((8, tf), F32)
        sv[0:8, :] = jnp.zeros((8, tf), F32)

    @pl.when(jnp.logical_not(seq_start))
    def _():
        sg[0:8, :] = carry[f, 0]
        sv[0:8, :] = carry[f, 1]

    sg[8:8 + tm, :] = ug
    sv[8:8 + tm, :] = uv
    carry[f, 0] = ug[tm - 8:tm, :]
    carry[f, 1] = uv[tm - 8:tm, :]
    cg = cbg_ref[...] + sg[6:6 + tm, :] * cwg_ref[0:1, :] + sg[7:7 + tm, :] * cwg_ref[1:2, :] + ug * cwg_ref[2:3, :]
    cv = cbv_ref[...] + sv[6:6 + tm, :] * cwv_ref[0:1, :] + sv[7:7 + tm, :] * cwv_ref[1:2, :] + uv * cwv_ref[2:3, :]
    contrib = _dot((_gelu(cg) * cv).astype(BF16), wd_ref[...])

    @pl.when(f == 0)
    def _():
        acc[...] = contrib

    @pl.when(f != 0)
    def _():
        acc[...] += contrib

    @pl.when(f == nf - 1)
    def _():
        o_ref[...] = x_ref[...] + acc[...]


def _ffn(x, g_norm, w_up, conv_w, conv_b, w_down, seq, tm=512, tf=512):
    m, d = x.shape
    nf = D_FF // tf
    cb = conv_b.reshape(1, 2 * D_FF)
    return pl.pallas_call(
        functools.partial(_ffn_kernel, blocks_per_seq=seq // tm),
        grid=(m // tm, nf),
        in_specs=[
            pl.BlockSpec((tm, d), lambda i, f: (i, 0)),
            pl.BlockSpec((1, d), lambda i, f: (0, 0)),
            pl.BlockSpec((d, tf), lambda i, f: (0, f)),
            pl.BlockSpec((d, tf), lambda i, f: (0, f + nf)),
            pl.BlockSpec((3, tf), lambda i, f: (0, f)),
            pl.BlockSpec((3, tf), lambda i, f: (0, f + nf)),
            pl.BlockSpec((1, tf), lambda i, f: (0, f)),
            pl.BlockSpec((1, tf), lambda i, f: (0, f + nf)),
            pl.BlockSpec((tf, d), lambda i, f: (f, 0)),
        ],
        out_specs=pl.BlockSpec((tm, d), lambda i, f: (i, 0)),
        out_shape=jax.ShapeDtypeStruct((m, d), F32),
        scratch_shapes=[
            pltpu.VMEM((tm, d), BF16),
            pltpu.VMEM((tm, d), F32),
            pltpu.VMEM((tm + 8, tf), F32),
            pltpu.VMEM((tm + 8, tf), F32),
            pltpu.VMEM((nf, 2, 8, tf), F32),
        ],
        compiler_params=_params(("arbitrary", "arbitrary")),
        name="conv_ffn",
    )(x, g_norm.reshape(1, d), w_up, w_up, conv_w, conv_w, cb, cb, w_down)


def _ple_kernel(x_ref, p_ref, gn_ref, wg_ref, wp_ref, gnext_ref, xo_ref, ho_ref):
    x = x_ref[...]
    hn = _rms(x, gn_ref[...]).astype(BF16)
    gate = _sigmoid(_dot(hn, wg_ref[...]))
    xn = x + gate * _dot(p_ref[...].astype(BF16), wp_ref[...])
    xo_ref[...] = xn
    ho_ref[...] = _rms(xn, gnext_ref[...]).astype(ho_ref.dtype)


def _ple(x, p, g_norm, w_gate, w_proj, g_next, next_dtype, tm=256):
    m, d = x.shape
    pd = p.shape[1]
    row = lambda i: (i, 0)
    const = lambda i: (0, 0)
    return pl.pallas_call(
        _ple_kernel,
        grid=(m // tm,),
        in_specs=[
            pl.BlockSpec((tm, d), row),
            pl.BlockSpec((tm, pd), row),
            pl.BlockSpec((1, d), const),
            pl.BlockSpec((d, d), const),
            pl.BlockSpec((pd, d), const),
            pl.BlockSpec((1, d), const),
        ],
        out_specs=[pl.BlockSpec((tm, d), row), pl.BlockSpec((tm, d), row)],
        out_shape=[jax.ShapeDtypeStruct((m, d), F32), jax.ShapeDtypeStruct((m, d), next_dtype)],
        compiler_params=_params(("arbitrary",)),
        name="ple_gate",
    )(x, p, g_norm.reshape(1, d), w_gate, w_proj, g_next.reshape(1, d))


def _permute_w_in(w):
    (a_z, a_xbc, a_dt, b_q, b_k, b_v, c_q, c_kc, c_vc, c_ks, c_vs, c_kw, c_vw, c_g,
     d_gate, d_x) = jnp.split(w, list(IN_OFFSETS), axis=1)
    k = w.shape[0]
    scale = HEAD ** -0.5
    gates = c_g.reshape(k, MIX_WIDTH // HEAD, 3).transpose(0, 2, 1).reshape(k, 48)
    padc = lambda a, n: jnp.pad(a, ((0, 0), (0, n - a.shape[1])))
    cols = [d_gate, d_x, a_z, b_q * scale, b_k, b_v, c_q * scale, a_xbc[:, :MIX_WIDTH], a_xbc[:, MIX_WIDTH:],
            c_ks, c_vs, c_kw, c_vw, c_kc, c_vc, padc(a_dt, LANES), padc(gates, LANES)]
    wp = jnp.concatenate(cols, axis=1)
    return padc(wp, PROJ_COLS).astype(BF16)


def _rope_tables(seq):
    half = HEAD // 2
    inv_freq = ROPE_THETA ** (-jnp.arange(half, dtype=F32) / half)
    ang = jnp.arange(seq).astype(F32)[:, None] * inv_freq[None, :]
    cos, sin = jnp.cos(ang), jnp.sin(ang)
    cos_t = jnp.tile(cos, (1, LANES // half))
    sin_t = jnp.tile(jnp.concatenate([-sin, sin], axis=1), (1, LANES // HEAD))
    return cos_t, sin_t


def _rope_flags():
    flags = np.zeros((PROJ_COLS // LANES,), np.int32)
    for off, width in ROPE_SEGMENTS:
        flags[off // LANES:(off + width) // LANES] = 1
    return jnp.asarray(flags)


def kernel(x, p, norm_mix, norm_ffn, norm_ple, w_in, ssd_conv_w, ssd_conv_b, ssd_dt_bias, ssd_a_log, ssd_d, ssd_norm, diff_lq1, diff_lk1, diff_lq2, diff_lk2, diff_norm, nsa_pos_cmp, nsa_ck_w1, nsa_ck_w2, nsa_cv_w1, nsa_cv_w2, rnn_conv_w, rnn_conv_b, rnn_w_r, rnn_b_r, rnn_w_i, rnn_b_i, rnn_lambda, w_merge_gate, w_branch, w_out, ffn_w_up, ffn_conv_w, ffn_conv_b, ffn_w_down, ple_w_proj, ple_w_gate, norm_final):
    bsz, seq, d = x.shape
    depth = w_in.shape[0]
    m = bsz * seq
    xf = x.reshape(m, d)
    cos_t, sin_t = _rope_tables(seq)
    flags = _rope_flags()
    h = _rmsnorm(xf, norm_mix[0], BF16)
    for l in range(depth):
        proj = _project(h, _permute_w_in(w_in[l]), flags, cos_t, sin_t, seq)
        o_a = _ssd(proj, bsz, seq, ssd_conv_w[l], ssd_conv_b[l], ssd_dt_bias[l], ssd_a_log[l], ssd_d[l], ssd_norm[l])
        o_b = _diff_attention(proj, bsz, seq, diff_lq1[l], diff_lk1[l], diff_lq2[l], diff_lk2[l], diff_norm[l],
                              0.8 - 0.6 * math.exp(-0.3 * l))
        cmp_kv = _compress(proj, bsz, seq, nsa_pos_cmp[l], nsa_ck_w1[l], nsa_ck_w2[l], nsa_cv_w1[l], nsa_cv_w2[l])
        o_c = _nsa_attention(proj, cmp_kv, bsz, seq)
        o_d = _rglru(proj, bsz, seq, rnn_conv_w[l], rnn_conv_b[l], rnn_w_r[l], rnn_b_r[l], rnn_w_i[l], rnn_b_i[l],
                     rnn_lambda[l])
        merged = _merge(h, (o_a, o_b, o_c, o_d), w_merge_gate[l].astype(BF16), w_branch[l].astype(BF16))
        xf = _out_proj(merged, w_out[l].astype(BF16), xf)
        xf = _ffn(xf, norm_ffn[l], ffn_w_up[l].astype(BF16), ffn_conv_w[l], ffn_conv_b[l],
                  ffn_w_down[l].astype(BF16), seq)
        last = l == depth - 1
        g_next = norm_final if last else norm_mix[l + 1]
        xf, h = _ple(xf, p[l].reshape(m, -1), norm_ple[l], ple_w_gate[l].astype(BF16), ple_w_proj[l].astype(BF16),
                     g_next, F32 if last else BF16)
    return h.reshape(bsz, seq, d)
```
